```python
import jax, jax.numpy as jnp
from jax import lax
import numpy as np

D_MODEL = 2048
BATCH = 16
SEQ = 2048
DEPTH = 4

N_META = 16
N_MIXERS = 2
D_FF = 5504
EPS = 1e-6
N_MLA_LAYERS = (DEPTH + 1) // 2
N_SSM_LAYERS = DEPTH // 2
MLA_HEADS = 16
Q_LORA = 512
KV_LORA = 512
QK_NOPE = 128
QK_ROPE = 64
V_HEAD = 128
ROPE_THETA = 10000.0
Q_BLOCK = 128
SSM_INNER = 2 * D_MODEL
SSM_HEAD_DIM = 64
SSM_HEADS = SSM_INNER // SSM_HEAD_DIM
SSM_GROUPS = 8
SSM_HEADS_PER_GROUP = SSM_HEADS // SSM_GROUPS
SSM_STATE = 128
SSM_CONV = 4
SSM_CHUNK = 128
SSM_CONV_DIM = SSM_INNER + 2 * SSM_GROUPS * SSM_STATE

kernel_name = 'hybrid_mla_ssd_macaron_meta'


def rms_norm(x, g):
    xf = x.astype(jnp.float32)
    y = xf * lax.rsqrt(jnp.mean(xf * xf, axis=-1, keepdims=True) + EPS)
    return (y * g.astype(jnp.float32)).astype(x.dtype)


def swiglu_ffn(h, w_in, w_out):
    gate, up = jnp.split(h @ w_in, 2, axis=-1)
    return (jax.nn.silu(gate) * up) @ w_out


def rope_tables(length):
    inv_freq = 1.0 / (ROPE_THETA ** (jnp.arange(0, QK_ROPE, 2, dtype=jnp.float32) / QK_ROPE))
    ang = jnp.arange(length, dtype=jnp.float32)[:, None] * inv_freq[None, :]
    return jnp.cos(ang), jnp.sin(ang)


def apply_rope(t, cos, sin):
    tf = t.astype(jnp.float32)
    t1, t2 = jnp.split(tf, 2, axis=-1)
    return jnp.concatenate([t1 * cos - t2 * sin, t2 * cos + t1 * sin], axis=-1).astype(t.dtype)


def query_blocks(length):
    return [(0, N_META)] + [(s, min(s + Q_BLOCK, length)) for s in range(N_META, length, Q_BLOCK)]


def mla_mixer(h, w_in, q_norm, w_uq, kv_norm, w_ukv, w_o):
    b, L, _ = h.shape
    c_q, c_kv, k_rope = jnp.split(h @ w_in, [Q_LORA, Q_LORA + KV_LORA], axis=-1)
    q = (rms_norm(c_q, q_norm) @ w_uq).reshape(b, L, MLA_HEADS, QK_NOPE + QK_ROPE)
    q_nope, q_rope = q[..., :QK_NOPE], q[..., QK_NOPE:]
    kv = (rms_norm(c_kv, kv_norm) @ w_ukv).reshape(b, L, MLA_HEADS, QK_NOPE + V_HEAD)
    k_nope, v = kv[..., :QK_NOPE], kv[..., QK_NOPE:]
    cos, sin = rope_tables(L)
    q_rope = apply_rope(q_rope, cos[:, None, :], sin[:, None, :])
    k_rope = apply_rope(k_rope, cos, sin)
    scale = (QK_NOPE + QK_ROPE) ** -0.5
    outs = []
    for start, end in query_blocks(L):
        s = (jnp.einsum('bqhd,bkhd->bhqk', q_nope[:, start:end], k_nope[:, :end])
             + jnp.einsum('bqhr,bkr->bhqk', q_rope[:, start:end], k_rope[:, :end])).astype(jnp.float32) * scale
        causal = jnp.arange(end)[None, :] <= jnp.arange(start, end)[:, None]
        p = jax.nn.softmax(jnp.where(causal, s, -jnp.inf), axis=-1)
        outs.append(jnp.einsum('bhqk,bkhd->bqhd', p.astype(v.dtype), v[:, :end]))
    o = jnp.concatenate(outs, axis=1).reshape(b, L, MLA_HEADS * V_HEAD)
    return o @ w_o


def causal_depthwise_conv(u, w, bias):
    out = lax.conv_general_dilated(u, w[:, None, :], window_strides=(1,), padding=[(SSM_CONV - 1, 0)],
                                   dimension_numbers=('NWC', 'WIO', 'NWC'), feature_group_count=u.shape[-1])
    return out + bias


def ssd_chunked_scan(xs, dt, a, bm, cm):
    b = xs.shape[0]
    pad_front = (-N_META) % SSM_CHUNK

    def chunks(t):
        t = t.astype(jnp.float32)
        t = jnp.pad(t, [(0, 0), (pad_front, 0)] + [(0, 0)] * (t.ndim - 2))
        return jnp.moveaxis(t.reshape(b, -1, SSM_CHUNK, *t.shape[2:]), 1, 0)

    causal = jnp.tril(jnp.ones((SSM_CHUNK, SSM_CHUNK), dtype=bool))[None, :, :, None, None]

    def step(state, inp):
        x_c, dt_c, b_c, c_c = inp
        a_cs = jnp.cumsum(dt_c * a, axis=1)
        seg = a_cs[:, :, None] - a_cs[:, None, :]
        decay = jnp.exp(jnp.where(causal, seg, -jnp.inf))
        xdt = x_c * dt_c[..., None]
        cb = jnp.einsum('blgn,bsgn->blsg', c_c, b_c)
        y_diag = jnp.einsum('blsgr,bsgrp->blgrp', cb[..., None] * decay, xdt)
        y_off = jnp.einsum('blgn,bgrpn->blgrp', c_c, state) * jnp.exp(a_cs)[..., None]
        to_end = jnp.exp(a_cs[:, -1:] - a_cs)
        new_state = (state * jnp.exp(a_cs[:, -1])[..., None, None]
                     + jnp.einsum('blgn,blgrp->bgrpn', b_c, xdt * to_end[..., None]))
        return new_state, y_diag + y_off

    state0 = jnp.zeros((b, SSM_GROUPS, SSM_HEADS_PER_GROUP, SSM_HEAD_DIM, SSM_STATE), jnp.float32)
    _, ys = lax.scan(step, state0, (chunks(xs), chunks(dt), chunks(bm), chunks(cm)))
    ys = jnp.moveaxis(ys, 0, 1).reshape(b, -1, SSM_GROUPS, SSM_HEADS_PER_GROUP, SSM_HEAD_DIM)
    return ys[:, pad_front:]


def mamba2_mixer(h, w_in, conv_w, conv_b, dt_bias, a_log, d_skip, norm_w, w_out):
    b, L, _ = h.shape
    G, R, P, N = SSM_GROUPS, SSM_HEADS_PER_GROUP, SSM_HEAD_DIM, SSM_STATE
    z, xbc, dt = jnp.split(h @ w_in, [SSM_INNER, SSM_INNER + SSM_CONV_DIM], axis=-1)
    xbc = jax.nn.silu(causal_depthwise_conv(xbc, conv_w, conv_b))
    xs, bm, cm = jnp.split(xbc, [SSM_INNER, SSM_INNER + G * N], axis=-1)
    xs = xs.reshape(b, L, G, R, P)
    bm = bm.reshape(b, L, G, N)
    cm = cm.reshape(b, L, G, N)
    dt = jax.nn.softplus((dt + dt_bias).astype(jnp.float32)).reshape(b, L, G, R)
    a = -jnp.exp(a_log.astype(jnp.float32)).reshape(G, R)
    y = ssd_chunked_scan(xs, dt, a, bm, cm) + d_skip.astype(jnp.float32).reshape(G, R)[..., None] * xs.astype(jnp.float32)
    y = y.reshape(b, L, SSM_INNER) * jax.nn.silu(z.astype(jnp.float32))
    y = rms_norm(y.reshape(b, L, G, SSM_INNER // G), norm_w.reshape(G, SSM_INNER // G))
    return y.reshape(b, L, SSM_INNER).astype(h.dtype) @ w_out


def setup_inputs(seed: int = 0) -> dict:
    key = jax.random.key(seed)
    ks = iter(jax.random.split(key, 32))
    f32 = jnp.float32

    def dense(shape, fan_in):
        return jax.random.normal(next(ks), shape, f32) * fan_in ** -0.5

    def gain(shape):
        return 1.0 + 0.02 * jax.random.normal(next(ks), shape, f32)

    nA, nB = N_MLA_LAYERS, N_SSM_LAYERS
    dt0 = jnp.exp(jax.random.uniform(next(ks), (nB, SSM_HEADS), f32, jnp.log(1e-3), jnp.log(1e-1)))
    return {
        'x': jax.random.normal(next(ks), (BATCH, SEQ, D_MODEL), f32),
        'meta_tokens': jax.random.normal(next(ks), (N_META, D_MODEL), f32),
        'norm_ffn1': gain((DEPTH, D_MODEL)),
        'ffn1_w_in': dense((DEPTH, D_MODEL, 2 * D_FF), D_MODEL),
        'ffn1_w_out': dense((DEPTH, D_FF, D_MODEL), D_FF),
        'norm_mix': gain((DEPTH, D_MODEL)),
        'norm_ffn2': gain((DEPTH, D_MODEL)),
        'ffn2_w_in': dense((DEPTH, D_MODEL, 2 * D_FF), D_MODEL),
        'ffn2_w_out': dense((DEPTH, D_FF, D_MODEL), D_FF),
        'mla_w_in': dense((nA, D_MODEL, Q_LORA + KV_LORA + QK_ROPE), D_MODEL),
        'mla_q_norm': gain((nA, Q_LORA)),
        'mla_w_uq': dense((nA, Q_LORA, MLA_HEADS * (QK_NOPE + QK_ROPE)), Q_LORA),
        'mla_kv_norm': gain((nA, KV_LORA)),
        'mla_w_ukv': dense((nA, KV_LORA, MLA_HEADS * (QK_NOPE + V_HEAD)), KV_LORA),
        'mla_w_o': dense((nA, MLA_HEADS * V_HEAD, D_MODEL), MLA_HEADS * V_HEAD),
        'ssm_w_in': dense((nB, D_MODEL, SSM_INNER + SSM_CONV_DIM + SSM_HEADS), D_MODEL),
        'ssm_conv_w': dense((nB, SSM_CONV, SSM_CONV_DIM), SSM_CONV),
        'ssm_conv_b': 0.02 * jax.random.normal(next(ks), (nB, SSM_CONV_DIM), f32),
        'ssm_dt_bias': dt0 + jnp.log(-jnp.expm1(-dt0)),
        'ssm_a_log': jnp.log(jax.random.uniform(next(ks), (nB, SSM_HEADS), f32, 1.0, 16.0)),
        'ssm_d': 1.0 + 0.1 * jax.random.normal(next(ks), (nB, SSM_HEADS), f32),
        'ssm_norm': gain((nB, SSM_INNER)),
        'ssm_w_out': dense((nB, SSM_INNER, D_MODEL), SSM_INNER),
        'final_norm': gain((D_MODEL,)),
    }


def reference(x, meta_tokens, norm_ffn1, ffn1_w_in, ffn1_w_out, norm_mix, norm_ffn2, ffn2_w_in, ffn2_w_out,
              mla_w_in, mla_q_norm, mla_w_uq, mla_kv_norm, mla_w_ukv, mla_w_o,
              ssm_w_in, ssm_conv_w, ssm_conv_b, ssm_dt_bias, ssm_a_log, ssm_d, ssm_norm, ssm_w_out,
              final_norm):
    b = x.shape[0]
    meta = jnp.broadcast_to(meta_tokens[None].astype(x.dtype), (b, N_META, x.shape[-1]))
    h = jnp.concatenate([meta, x], axis=1)
    for i in range(DEPTH):
        h = h + 0.5 * swiglu_ffn(rms_norm(h, norm_ffn1[i]), ffn1_w_in[i], ffn1_w_out[i])
        u = rms_norm(h, norm_mix[i])
        j = i // N_MIXERS
        if i % N_MIXERS == 0:
            h = h + mla_mixer(u, mla_w_in[j], mla_q_norm[j], mla_w_uq[j], mla_kv_norm[j], mla_w_ukv[j], mla_w_o[j])
        else:
            h = h + mamba2_mixer(u, ssm_w_in[j], ssm_conv_w[j], ssm_conv_b[j], ssm_dt_bias[j], ssm_a_log[j],
                                 ssm_d[j], ssm_norm[j], ssm_w_out[j])
        h = h + 0.5 * swiglu_ffn(rms_norm(h, norm_ffn2[i]), ffn2_w_in[i], ffn2_w_out[i])
    return rms_norm(h, final_norm)[:, N_META:]
```

```python
import functools
import math

import jax
import jax.numpy as jnp
from jax import lax
from jax.experimental import pallas as pl
from jax.experimental.pallas import tpu as pltpu

F32 = jnp.float32
BF16 = jnp.bfloat16

EPS = 1e-6
N_META = 16
QK_NOPE = 128
QK_ROPE = 64
V_HEAD = 128
ROPE_THETA = 10000.0
SSM_GROUPS = 8
SSM_STATE = 128
SSM_CONV = 4
SSM_CHUNK = 128

LANES = 128
VMEM_LIMIT_BYTES = 56 * 1024 * 1024


def _round_up(x, m):
    return (x + m - 1) // m * m


def _pick_tile(n, candidates):
    for c in candidates:
        if n % c == 0:
            return c
    raise ValueError(f"no tile for {n} among {candidates}")


def _params(semantics):
    return pltpu.CompilerParams(dimension_semantics=semantics, vmem_limit_bytes=VMEM_LIMIT_BYTES)


def _rms(x, g):
    ms = jnp.mean(x * x, axis=-1, keepdims=True)
    return x * lax.rsqrt(ms + EPS) * g


def _silu(x):
    return x * jax.nn.sigmoid(x)


def _ffn_body(h_ref, g_ref, wg_ref, wu_ref, wo_ref, o_ref, xn_ref):
    j = pl.program_id(1)

    @pl.when(j == 0)
    def _():
        xn_ref[...] = _rms(h_ref[...], g_ref[...]).astype(BF16)
        o_ref[...] = jnp.zeros_like(o_ref)

    xn = xn_ref[...]
    gate = jnp.dot(xn, wg_ref[...], preferred_element_type=F32)
    up = jnp.dot(xn, wu_ref[...], preferred_element_type=F32)
    act = (_silu(gate) * up).astype(BF16)
    o_ref[...] += jnp.dot(act, wo_ref[...], preferred_element_type=F32)

    @pl.when(j == pl.num_programs(1) - 1)
    def _():
        o_ref[...] = h_ref[...] + 0.5 * o_ref[...]


def _ffn(h, g, wg, wu, wo, layer, tm, tf):
    T, D = h.shape
    ffp = wg.shape[-1]
    return pl.pallas_call(
        _ffn_body,
        grid=(T // tm, ffp // tf),
        in_specs=[
            pl.BlockSpec((tm, D), lambda i, j: (i, 0)),
            pl.BlockSpec((None, 1, D), lambda i, j: (layer, 0, 0)),
            pl.BlockSpec((None, D, tf), lambda i, j: (layer, 0, j)),
            pl.BlockSpec((None, D, tf), lambda i, j: (layer, 0, j)),
            pl.BlockSpec((None, tf, D), lambda i, j: (layer, j, 0)),
        ],
        out_specs=pl.BlockSpec((tm, D), lambda i, j: (i, 0)),
        out_shape=jax.ShapeDtypeStruct((T, D), F32),
        scratch_shapes=[pltpu.VMEM((tm, D), BF16)],
        compiler_params=_params(("parallel", "arbitrary")),
        name="ffn",
    )(h, g, wg, wu, wo)


def _proj_res_body(h_ref, x_ref, w_ref, o_ref):
    k = pl.program_id(1)

    @pl.when(k == 0)
    def _():
        o_ref[...] = h_ref[...]

    o_ref[...] += jnp.dot(x_ref[...], w_ref[...], preferred_element_type=F32)


def _proj_res(h, x, w, layer, tm, tk):
    T, D = h.shape
    K = x.shape[1]
    return pl.pallas_call(
        _proj_res_body,
        grid=(T // tm, K // tk),
        in_specs=[
            pl.BlockSpec((tm, D), lambda i, k: (i, 0)),
            pl.BlockSpec((tm, tk), lambda i, k: (i, k)),
            pl.BlockSpec((None, tk, D), lambda i, k: (layer, k, 0)),
        ],
        out_specs=pl.BlockSpec((tm, D), lambda i, k: (i, 0)),
        out_shape=jax.ShapeDtypeStruct((T, D), F32),
        compiler_params=_params(("parallel", "arbitrary")),
        name="proj_res",
    )(h, x, w)


def _mla_in_body(scale, nq_nope, nq_rope, h_ref, g_ref, win_ref, qg_ref, kvg_ref, wuq_ref, wukv_ref,
                 cos_ref, sin_ref, qn_ref, qr_ref, kr_ref, kv_ref):
    u = _rms(h_ref[...], g_ref[...]).astype(BF16)
    lat = jnp.dot(u, win_ref[...], preferred_element_type=F32)
    ql = qg_ref.shape[-1]
    kvl = kvg_ref.shape[-1]
    cos = cos_ref[...]
    sin = sin_ref[...]
    kr = lat[:, ql + kvl:ql + kvl + LANES] * cos + lat[:, ql + kvl + LANES:ql + kvl + 2 * LANES] * sin
    kr_ref[...] = kr.astype(BF16)
    cq = _rms(lat[:, :ql], qg_ref[...]).astype(BF16)
    ckv = _rms(lat[:, ql:ql + kvl], kvg_ref[...]).astype(BF16)
    kv_ref[...] = jnp.dot(ckv, wukv_ref[...], preferred_element_type=F32).astype(BF16)
    q = jnp.dot(cq, wuq_ref[...], preferred_element_type=F32) * scale
    qn_ref[...] = q[:, :nq_nope].astype(BF16)
    for s in range(nq_rope // LANES):
        a = q[:, nq_nope + s * LANES:nq_nope + (s + 1) * LANES]
        b = q[:, nq_nope + nq_rope + s * LANES:nq_nope + nq_rope + (s + 1) * LANES]
        qr_ref[:, s * LANES:(s + 1) * LANES] = (a * cos + b * sin).astype(BF16)


def _mla_in(h, g, win, qg, kvg, wuq, wukv, cos, sin, depth_idx, layer, tm, n_heads):
    T, D = h.shape
    nq_nope = n_heads * QK_NOPE
    nq_rope = n_heads * QK_ROPE
    scale = (QK_NOPE + QK_ROPE) ** -0.5
    ql, kvl = qg.shape[-1], kvg.shape[-1]
    nkv = wukv.shape[-1]

    def full(a, idx=layer):
        return pl.BlockSpec((None,) + a.shape[1:], lambda i: (idx,) + (0,) * (a.ndim - 1))

    return pl.pallas_call(
        functools.partial(_mla_in_body, scale, nq_nope, nq_rope),
        grid=(T // tm,),
        in_specs=[
            pl.BlockSpec((tm, D), lambda i: (i, 0)),
            full(g, depth_idx), full(win), full(qg), full(kvg), full(wuq), full(wukv),
            pl.BlockSpec((tm, LANES), lambda i: (i, 0)),
            pl.BlockSpec((tm, LANES), lambda i: (i, 0)),
        ],
        out_specs=[
            pl.BlockSpec((tm, nq_nope), lambda i: (i, 0)),
            pl.BlockSpec((tm, nq_rope), lambda i: (i, 0)),
            pl.BlockSpec((tm, LANES), lambda i: (i, 0)),
            pl.BlockSpec((tm, nkv), lambda i: (i, 0)),
        ],
        out_shape=[
            jax.ShapeDtypeStruct((T, nq_nope), BF16),
            jax.ShapeDtypeStruct((T, nq_rope), BF16),
            jax.ShapeDtypeStruct((T, LANES), BF16),
            jax.ShapeDtypeStruct((T, nkv), BF16),
        ],
        compiler_params=_params(("parallel",)),
        name="mla_in",
    )(h, g, win, qg, kvg, wuq, wukv, cos, sin)


def _head_q(qn, qr, x):
    lane = lax.broadcasted_iota(jnp.int32, (1, LANES), 1)
    mine = (lane >= x * QK_ROPE) & (lane < (x + 1) * QK_ROPE)
    qrx = jnp.where(mine, qr, jnp.zeros_like(qr))
    return jnp.concatenate([qn[:, x * QK_NOPE:(x + 1) * QK_NOPE], qrx], axis=1)


def _head_k(kv, kr, x):
    return jnp.concatenate([kv[:, x * 2 * LANES:x * 2 * LANES + QK_NOPE], kr], axis=1)


def _head_v(kv, x):
    return kv[:, x * 2 * LANES + QK_NOPE:(x + 1) * 2 * LANES]


def _qk(q, k):
    return lax.dot_general(q, k, (((1,), (1,)), ((), ())), preferred_element_type=F32)


def _attn_body(tq, qi_tab, kj_tab, qn_ref, qr_ref, kv_ref, kr_ref, kvm_ref, krm_ref, o_ref, m_ref, l_ref, acc_ref):
    t = pl.program_id(2)
    qi = qi_tab[t]
    kj = kj_tab[t]
    qn = qn_ref[...]
    qr = qr_ref[...]
    kv = kv_ref[...]
    kr = kr_ref[...]

    @pl.when(kj == 0)
    def _():
        kvm = kvm_ref[...]
        krm = krm_ref[...]
        for x in range(2):
            s = _qk(_head_q(qn, qr, x), _head_k(kvm, krm, x))
            m0 = jnp.max(s, axis=-1, keepdims=True)
            p = jnp.exp(s - m0)
            m_ref[x] = m0
            l_ref[x] = jnp.sum(p, axis=-1, keepdims=True)
            acc_ref[x] = jnp.dot(p.astype(BF16), _head_v(kvm, x), preferred_element_type=F32)

    r = lax.broadcasted_iota(jnp.int32, (tq, tq), 0)
    c = lax.broadcasted_iota(jnp.int32, (tq, tq), 1)
    visible = (c - r) <= (qi - kj) * tq
    for x in range(2):
        s = _qk(_head_q(qn, qr, x), _head_k(kv, kr, x))
        s = jnp.where(visible, s, -jnp.inf)
        m = m_ref[x]
        m_new = jnp.maximum(m, jnp.max(s, axis=-1, keepdims=True))
        alpha = jnp.exp(m - m_new)
        p = jnp.exp(s - m_new)
        m_ref[x] = m_new
        l_ref[x] = alpha * l_ref[x] + jnp.sum(p, axis=-1, keepdims=True)
        acc_ref[x] = alpha * acc_ref[x] + jnp.dot(p.astype(BF16), _head_v(kv, x), preferred_element_type=F32)

    @pl.when(kj == qi)
    def _():
        for x in range(2):
            o_ref[:, x * V_HEAD:(x + 1) * V_HEAD] = (acc_ref[x] / l_ref[x]).astype(o_ref.dtype)


def _attn_meta_body(qn_ref, qr_ref, kv_ref, kr_ref, o_ref):
    qn = qn_ref[...]
    qr = qr_ref[...]
    kv = kv_ref[...]
    kr = kr_ref[...]
    r = lax.broadcasted_iota(jnp.int32, (N_META, N_META), 0)
    c = lax.broadcasted_iota(jnp.int32, (N_META, N_META), 1)
    for x in range(2):
        s = _qk(_head_q(qn, qr, x), _head_k(kv, kr, x))
        s = jnp.where(c <= r, s, -jnp.inf)
        m = jnp.max(s, axis=-1, keepdims=True)
        p = jnp.exp(s - m)
        l = jnp.sum(p, axis=-1, keepdims=True)
        o = jnp.dot(p.astype(BF16), _head_v(kv, x), preferred_element_type=F32)
        o_ref[:, x * V_HEAD:(x + 1) * V_HEAD] = (o / l).astype(o_ref.dtype)


def _attention(qn, qr, kr, kv, batch, seq, n_heads, tq):
    T = qn.shape[0]
    n_main = batch * seq
    nq = seq // tq
    hp = n_heads // 2
    meta_blk = n_main // N_META
    pairs = [(i, j) for i in range(nq) for j in range(i + 1)]
    qi_tab = jnp.asarray([p[0] for p in pairs], jnp.int32)
    kj_tab = jnp.asarray([p[1] for p in pairs], jnp.int32)

    def q_idx(b, p, t, qt, kt):
        return (b * nq + qt[t], p)

    o_main = pl.pallas_call(
        functools.partial(_attn_body, tq),
        grid_spec=pltpu.PrefetchScalarGridSpec(
            num_scalar_prefetch=2,
            grid=(batch, hp, len(pairs)),
            in_specs=[
                pl.BlockSpec((tq, 2 * QK_NOPE), q_idx),
                pl.BlockSpec((tq, LANES), q_idx),
                pl.BlockSpec((tq, 4 * LANES), lambda b, p, t, qt, kt: (b * nq + kt[t], p)),
                pl.BlockSpec((tq, LANES), lambda b, p, t, qt, kt: (b * nq + kt[t], 0)),
                pl.BlockSpec((N_META, 4 * LANES), lambda b, p, t, qt, kt: (meta_blk + b, p)),
                pl.BlockSpec((N_META, LANES), lambda b, p, t, qt, kt: (meta_blk + b, 0)),
            ],
            out_specs=pl.BlockSpec((tq, 2 * V_HEAD), q_idx),
            scratch_shapes=[
                pltpu.VMEM((2, tq, 1), F32),
                pltpu.VMEM((2, tq, 1), F32),
                pltpu.VMEM((2, tq, V_HEAD), F32),
            ],
        ),
        out_shape=jax.ShapeDtypeStruct((T, n_heads * V_HEAD), BF16),
        compiler_params=_params(("parallel", "parallel", "arbitrary")),
        name="attention",
    )(qi_tab, kj_tab, qn, qr, kv, kr, kv, kr)

    o_meta = pl.pallas_call(
        _attn_meta_body,
        grid=(batch, hp),
        in_specs=[
            pl.BlockSpec((N_META, 2 * QK_NOPE), lambda b, p: (meta_blk + b, p)),
            pl.BlockSpec((N_META, LANES), lambda b, p: (meta_blk + b, p)),
            pl.BlockSpec((N_META, 4 * LANES), lambda b, p: (meta_blk + b, p)),
            pl.BlockSpec((N_META, LANES), lambda b, p: (meta_blk + b, 0)),
        ],
        out_specs=pl.BlockSpec((N_META, 2 * V_HEAD), lambda b, p: (b, p)),
        out_shape=jax.ShapeDtypeStruct((batch * N_META, n_heads * V_HEAD), BF16),
        compiler_params=_params(("parallel", "parallel")),
        name="attention_meta",
    )(qn, qr, kv, kr)
    return lax.dynamic_update_slice(o_main, o_meta, (n_main, 0))


def _ssm_in_body(h_ref, g_ref, w_ref, wdt_ref, dtb_ref, zx_ref, dt_ref, xn_ref):
    j = pl.program_id(1)

    @pl.when(j == 0)
    def _():
        xn = _rms(h_ref[...], g_ref[...]).astype(BF16)
        xn_ref[...] = xn
        dt = jnp.dot(xn, wdt_ref[...], preferred_element_type=F32) + dtb_ref[...]
        dt_ref[...] = jax.nn.softplus(dt)

    zx_ref[...] = jnp.dot(xn_ref[...], w_ref[...], preferred_element_type=F32).astype(BF16)


def _ssm_in(h, g, w, wdt, dtb, depth_idx, layer, tm, tn):
    T, D = h.shape
    n_zx = w.shape[-1]
    return pl.pallas_call(
        _ssm_in_body,
        grid=(T // tm, n_zx // tn),
        in_specs=[
            pl.BlockSpec((tm, D), lambda i, j: (i, 0)),
            pl.BlockSpec((None, 1, D), lambda i, j: (depth_idx, 0, 0)),
            pl.BlockSpec((None, D, tn), lambda i, j: (layer, 0, j)),
            pl.BlockSpec((None, D, LANES), lambda i, j: (layer, 0, 0)),
            pl.BlockSpec((None, 1, LANES), lambda i, j: (layer, 0, 0)),
        ],
        out_specs=[
            pl.BlockSpec((tm, tn), lambda i, j: (i, j)),
            pl.BlockSpec((tm, LANES), lambda i, j: (i, 0)),
        ],
        out_shape=[
            jax.ShapeDtypeStruct((T, n_zx), BF16),
            jax.ShapeDtypeStruct((T, LANES), F32),
        ],
        scratch_shapes=[pltpu.VMEM((tm, D), BF16)],
        compiler_params=_params(("parallel", "arbitrary")),
        name="ssm_in",
    )(h, g, w, wdt, dtb)


def _split3(x):
    p1 = x.astype(BF16)
    r1 = x - p1.astype(F32)
    p2 = r1.astype(BF16)
    p3 = (r1 - p2.astype(F32)).astype(BF16)
    return p1, p2, p3


def _ssd_body(inner, n_heads, zx_ref, dt_ref, zxm_ref, dtm_ref, cw_ref, cb_ref, alog_ref, dskip_ref, nw_ref,
              y_ref, ym_ref, st_ref, ext_ref):
    c = pl.program_id(1)
    L = SSM_CHUNK
    N = SSM_STATE
    G = SSM_GROUPS
    P = inner // n_heads
    hpg = n_heads // G
    gw = inner // G
    n_xbc = inner + 2 * G * N
    pad = L - N_META
    is_meta = c == 0

    zx_meta = jnp.concatenate([jnp.zeros((pad, zxm_ref.shape[1]), BF16), zxm_ref[...]], axis=0)
    dt_meta = jnp.concatenate([jnp.zeros((pad, LANES), F32), dtm_ref[...]], axis=0)
    zx = jnp.where(is_meta, zx_meta, zx_ref[...])
    dt = jnp.where(is_meta, dt_meta, dt_ref[...])

    @pl.when(is_meta)
    def _():
        st_ref[...] = jnp.zeros_like(st_ref)
        ext_ref[0:8, :] = jnp.zeros((8, n_xbc), F32)

    ext_ref[8:8 + L, :] = zx[:, inner:inner + n_xbc].astype(F32)
    cw = cw_ref[...]
    xbc_cols = []
    CW = 512
    for j in range(n_xbc // CW):
        sl = slice(j * CW, (j + 1) * CW)
        acc = cb_ref[:, sl] + cw[0:1, sl] * ext_ref[5:5 + L, sl]
        for k in range(1, SSM_CONV):
            acc = acc + cw[k:k + 1, sl] * ext_ref[5 + k:5 + k + L, sl]
        xbc_cols.append(_silu(acc))
    ext_ref[0:8, :] = ext_ref[L:L + 8, :]

    def xbc(lo, hi):
        j0, j1 = lo // CW, (hi - 1) // CW
        assert j0 == j1
        return xbc_cols[j0][:, lo - j0 * CW:hi - j0 * CW]

    a_row = -jnp.exp(alog_ref[...])
    dta = dt * a_row
    row = lax.broadcasted_iota(jnp.int32, (L, L), 0)
    col = lax.broadcasted_iota(jnp.int32, (L, L), 1)
    tri = col <= row
    tri_b = jnp.where(tri, 1.0, 0.0).astype(BF16)
    acs = sum(jnp.dot(tri_b, p, preferred_element_type=F32) for p in _split3(dta))
    acs_t = acs.T
    dt_t = dt.T
    lane = lax.broadcasted_iota(jnp.int32, (1, LANES), 1)
    hpp = LANES // P
    lane_masks = [(lane >= x * P) & (lane < (x + 1) * P) for x in range(hpp)]

    for g in range(G):
        b_g = xbc(inner + g * N, inner + (g + 1) * N)
        c_g = xbc(inner + G * N + g * N, inner + G * N + (g + 1) * N)
        b_gb = b_g.astype(BF16)
        c_gb = c_g.astype(BF16)
        cb = lax.dot_general(c_gb, b_gb, (((1,), (1,)), ((), ())), preferred_element_type=F32)
        b_gt = b_g.T
        y_slabs = []
        for sidx in range(gw // LANES):
            ch0 = g * gw + sidx * LANES
            x_slab = xbc(ch0, ch0 + LANES)
            st_slab = st_ref[:, ch0:ch0 + LANES]
            y = dskip_ref[:, ch0:ch0 + LANES] * x_slab
            st_new = None
            for x in range(hpp):
                hd = ch0 // P + x
                colb = jnp.broadcast_to(acs[:, hd:hd + 1], (L, L))
                rowb = jnp.broadcast_to(acs_t[hd:hd + 1, :], (L, L))
                dtr = jnp.broadcast_to(dt_t[hd:hd + 1, :], (L, L))
                a_end = jnp.broadcast_to(acs_t[hd:hd + 1, L - 1:L], (L, L))
                decay = jnp.exp(jnp.where(tri, colb - rowb, -jnp.inf))
                lm = (cb * decay * dtr).astype(BF16)
                ce = (c_g * jnp.exp(colb)).astype(BF16)
                lhs = jnp.concatenate([lm, ce], axis=1)
                xm = jnp.where(lane_masks[x], x_slab, 0.0).astype(BF16)
                sm = jnp.where(lane_masks[x], st_slab, 0.0).astype(BF16)
                rhs = jnp.concatenate([xm, sm], axis=0)
                y = y + jnp.dot(lhs, rhs, preferred_element_type=F32)
                bw = (b_gt * (dtr * jnp.exp(a_end - rowb))).astype(BF16)
                upd = jnp.dot(bw, xm, preferred_element_type=F32)
                keep = jnp.where(lane_masks[x], st_slab * jnp.exp(a_end), 0.0)
                st_new = keep + upd if st_new is None else st_new + keep + upd
            st_ref[:, ch0:ch0 + LANES] = st_new
            y_slabs.append(y)
        yg = jnp.concatenate(y_slabs, axis=1)
        zg = zx[:, g * gw:(g + 1) * gw].astype(F32)
        yg = yg * _silu(zg)
        yn = _rms(yg, nw_ref[:, g * gw:(g + 1) * gw]).astype(BF16)

        @pl.when(is_meta)
        def _():
            ym_ref[:, g * gw:(g + 1) * gw] = yn[pad:, :]

        @pl.when(jnp.logical_not(is_meta))
        def _():
            y_ref[:, g * gw:(g + 1) * gw] = yn


def _ssd(zx, dt, cw, cb, alog, dskip, nw, layer, batch, seq, inner, n_heads):
    T = zx.shape[0]
    n_main = batch * seq
    nc = seq // SSM_CHUNK
    meta_blk = n_main // N_META
    n_xbc = inner + 2 * SSM_GROUPS * SSM_STATE
    n_zx = zx.shape[1]

    def main_idx(b, c):
        return (b * nc + jnp.maximum(c - 1, 0), 0)

    def par(a):
        return pl.BlockSpec((None,) + a.shape[1:], lambda b, c: (layer,) + (0,) * (a.ndim - 1))

    y_main, y_meta = pl.pallas_call(
        functools.partial(_ssd_body, inner, n_heads),
        grid=(batch, nc + 1),
        in_specs=[
            pl.BlockSpec((SSM_CHUNK, n_zx), main_idx),
            pl.BlockSpec((SSM_CHUNK, LANES), main_idx),
            pl.BlockSpec((N_META, n_zx), lambda b, c: (meta_blk + b, 0)),
            pl.BlockSpec((N_META, LANES), lambda b, c: (meta_blk + b, 0)),
            par(cw), par(cb), par(alog), par(dskip), par(nw),
        ],
        out_specs=[
            pl.BlockSpec((SSM_CHUNK, inner), main_idx),
            pl.BlockSpec((N_META, inner), lambda b, c: (b, 0)),
        ],
        out_shape=[
            jax.ShapeDtypeStruct((T, inner), BF16),
            jax.ShapeDtypeStruct((batch * N_META, inner), BF16),
        ],
        scratch_shapes=[
            pltpu.VMEM((SSM_STATE, inner), F32),
            pltpu.VMEM((SSM_CHUNK + 8, n_xbc), F32),
        ],
        compiler_params=_params(("parallel", "arbitrary")),
        name="ssd",
    )(zx, dt, zx, dt, cw, cb, alog, dskip, nw)
    return lax.dynamic_update_slice(y_main, y_meta, (n_main, 0))


def _final_norm_body(h_ref, g_ref, o_ref):
    o_ref[...] = _rms(h_ref[...], g_ref[...])


def _final_norm(h, g, n_main, tm):
    D = h.shape[1]
    return pl.pallas_call(
        _final_norm_body,
        grid=(n_main // tm,),
        in_specs=[pl.BlockSpec((tm, D), lambda i: (i, 0)), pl.BlockSpec((1, D), lambda i: (0, 0))],
        out_specs=pl.BlockSpec((tm, D), lambda i: (i, 0)),
        out_shape=jax.ShapeDtypeStruct((n_main, D), F32),
        compiler_params=_params(("parallel",)),
        name="final_norm",
    )(h, g)


def _rot_half_cols(w):
    half = QK_ROPE // 2
    blocks = w.reshape(w.shape[:-1] + (-1, QK_ROPE))
    sw = jnp.concatenate([-blocks[..., half:], blocks[..., :half]], axis=-1)
    return sw.reshape(w.shape)


def _prep_ffn(w_in, w_out, ffp):
    d_ff = w_out.shape[1]
    padc = [(0, 0), (0, 0), (0, ffp - d_ff)]
    wg = jnp.pad(w_in[:, :, :d_ff].astype(BF16), padc)
    wu = jnp.pad(w_in[:, :, d_ff:].astype(BF16), padc)
    wo = jnp.pad(w_out.astype(BF16), [(0, 0), (0, ffp - d_ff), (0, 0)])
    return wg, wu, wo


def _prep_mla(w_in, w_uq, n_heads, ql, kvl):
    kr = w_in[:, :, ql + kvl:]
    krs = _rot_half_cols(kr)
    win = jnp.concatenate([w_in[:, :, :ql + kvl], kr, kr, krs, krs], axis=-1).astype(BF16)
    nl = w_uq.shape[0]
    wq = w_uq.reshape(nl, ql, n_heads, QK_NOPE + QK_ROPE)
    nope = wq[..., :QK_NOPE].reshape(nl, ql, n_heads * QK_NOPE)
    rope = wq[..., QK_NOPE:].reshape(nl, ql, n_heads * QK_ROPE)
    wuq = jnp.concatenate([nope, rope, _rot_half_cols(rope)], axis=-1).astype(BF16)
    return win, wuq


def _rope_tables(batch, seq):
    inv_freq = 1.0 / (ROPE_THETA ** (jnp.arange(0, QK_ROPE, 2, dtype=F32) / QK_ROPE))
    pos_main = N_META + jnp.tile(jnp.arange(seq, dtype=F32), batch)
    pos_meta = jnp.tile(jnp.arange(N_META, dtype=F32), batch)
    ang = jnp.concatenate([pos_main, pos_meta])[:, None] * inv_freq[None, :]
    reps = LANES // (QK_ROPE // 2)
    return jnp.tile(jnp.cos(ang), (1, reps)), jnp.tile(jnp.sin(ang), (1, reps))


def kernel(x, meta_tokens, norm_ffn1, ffn1_w_in, ffn1_w_out, norm_mix, norm_ffn2, ffn2_w_in, ffn2_w_out,
           mla_w_in, mla_q_norm, mla_w_uq, mla_kv_norm, mla_w_ukv, mla_w_o,
           ssm_w_in, ssm_conv_w, ssm_conv_b, ssm_dt_bias, ssm_a_log, ssm_d, ssm_norm, ssm_w_out,
           final_norm):
    batch, seq, d_model = x.shape
    depth = norm_ffn1.shape[0]
    d_ff = ffn1_w_out.shape[1]
    ql, kvl = mla_q_norm.shape[1], mla_kv_norm.shape[1]
    n_heads = mla_w_uq.shape[2] // (QK_NOPE + QK_ROPE)
    inner = ssm_norm.shape[1]
    ssm_heads = ssm_a_log.shape[1]
    p_head = inner // ssm_heads
    n_xbc = inner + 2 * SSM_GROUPS * SSM_STATE
    assert meta_tokens.shape[0] == N_META and seq % SSM_CHUNK == 0 and n_heads % 2 == 0
    assert LANES % p_head == 0 and (inner // SSM_GROUPS) % LANES == 0 and ssm_heads <= LANES

    n_main = batch * seq
    T = n_main + batch * N_META
    tm = _pick_tile(T, (768, 384, 256, 128, 64, 32, 16))
    tf = 512 if d_ff >= 512 else LANES
    ffp = _round_up(d_ff, tf)
    tq = _pick_tile(seq, (512, 256, 128))

    meta = jnp.broadcast_to(meta_tokens[None].astype(x.dtype), (batch, N_META, d_model))
    h = jnp.concatenate([x.reshape(n_main, d_model), meta.reshape(batch * N_META, d_model)], axis=0)

    g1 = norm_ffn1[:, None, :]
    g2 = norm_ffn2[:, None, :]
    gm = norm_mix[:, None, :]
    f1 = _prep_ffn(ffn1_w_in, ffn1_w_out, ffp)
    f2 = _prep_ffn(ffn2_w_in, ffn2_w_out, ffp)

    mla_win, mla_wuq = _prep_mla(mla_w_in, mla_w_uq, n_heads, ql, kvl)
    mla_wukv = mla_w_ukv.astype(BF16)
    mla_wo = mla_w_o.astype(BF16)
    mla_qg = mla_q_norm[:, None, :]
    mla_kvg = mla_kv_norm[:, None, :]
    cos, sin = _rope_tables(batch, seq)

    n_zx = inner + n_xbc
    ssm_w_zx = ssm_w_in[:, :, :n_zx].astype(BF16)
    ssm_w_dt = jnp.pad(ssm_w_in[:, :, n_zx:].astype(BF16), [(0, 0), (0, 0), (0, LANES - ssm_heads)])
    padh = [(0, 0), (0, 0), (0, LANES - ssm_heads)]
    ssm_dtb = jnp.pad(ssm_dt_bias[:, None, :], padh)
    ssm_alog = jnp.pad(ssm_a_log[:, None, :], padh)
    ssm_dskip = jnp.repeat(ssm_d, p_head, axis=1)[:, None, :]
    ssm_cb = ssm_conv_b[:, None, :]
    ssm_nw = ssm_norm[:, None, :]
    ssm_wout = ssm_w_out.astype(BF16)
    tn = _pick_tile(n_zx, (1024, 512, 256, 128))
    tm_mla = _pick_tile(T, (384, 256, 128, 64, 32, 16))

    for i in range(depth):
        h = _ffn(h, g1, *f1, layer=i, tm=tm, tf=tf)
        j = i // 2
        if i % 2 == 0:
            qn, qr, kr, kv = _mla_in(h, gm, mla_win, mla_qg, mla_kvg, mla_wuq, mla_wukv, cos, sin,
                                     depth_idx=i, layer=j, tm=tm_mla, n_heads=n_heads)
            o = _attention(qn, qr, kr, kv, batch, seq, n_heads, tq)
            h = _proj_res(h, o, mla_wo, layer=j, tm=tm, tk=_pick_tile(o.shape[1], (1024, 512, 256, 128)))
        else:
            zx, dt = _ssm_in(h, gm, ssm_w_zx, ssm_w_dt, ssm_dtb, depth_idx=i, layer=j, tm=tm, tn=tn)
            y = _ssd(zx, dt, ssm_conv_w, ssm_cb, ssm_alog, ssm_dskip, ssm_nw, j, batch, seq, inner, ssm_heads)
            h = _proj_res(h, y, ssm_wout, layer=j, tm=tm, tk=_pick_tile(inner, (1024, 512, 256, 128)))
        h = _ffn(h, g2, *f2, layer=i, tm=tm, tf=tf)

    out = _final_norm(h, final_norm[None, :], n_main, _pick_tile(n_main, (1024, 512, 256, 128)))
    return out.reshape(batch, seq, d_model)
```
